```python
import jax, jax.numpy as jnp
from jax import lax
import numpy as np

D_MODEL = 1024
BATCH = 8
SEQ = 8192
DEPTH = 1

N_MEM = 256
EPS = 1e-6
NEG = -1e30
ROPE_THETA = 10000.0
MOBA_HEADS = 8
MOBA_HD = 64
MOBA_WIDTH = MOBA_HEADS * MOBA_HD
MOBA_BLOCK = 256
MOBA_TOPK = 3
Q_CHUNK = 128
GMLP_GROUPS = 8
GMLP_GD = 64
GMLP_WIDTH = GMLP_GROUPS * GMLP_GD
GMLP_CHUNK = 128
N_BRANCH = 2
IN_COLS = 3 * MOBA_WIDTH + 2 * GMLP_WIDTH + N_BRANCH * D_MODEL
XA_HEADS = 4
XA_HD = 128
XA_WIDTH = XA_HEADS * XA_HD
PEER_HEADS = 8
PEER_NKEYS = 128
PEER_N = PEER_NKEYS * PEER_NKEYS
PEER_DQ = 256
PEER_TOPK = 16
PEER_TOK_CHUNK = 128

kernel_name = "moba_gmlp_gated_peer_hybrid"


def rmsnorm(x, g):
    xf = x.astype(jnp.float32)
    y = xf * lax.rsqrt(jnp.mean(xf * xf, axis=-1, keepdims=True) + EPS)
    return (y * g.astype(jnp.float32)).astype(x.dtype)


def rope(x):
    S, Dh = x.shape[1], x.shape[3]
    half = Dh // 2
    freqs = ROPE_THETA ** (-jnp.arange(half, dtype=jnp.float32) / half)
    ang = jnp.arange(S, dtype=jnp.float32)[:, None] * freqs[None, :]
    c = jnp.cos(ang)[None, :, None, :]
    s = jnp.sin(ang)[None, :, None, :]
    xf = x.astype(jnp.float32)
    x1, x2 = xf[..., :half], xf[..., half:]
    return jnp.concatenate([x1 * c - x2 * s, x2 * c + x1 * s], axis=-1).astype(x.dtype)


def moba_attention(q, k, v):
    B, S, H, Dh = q.shape
    nb = -(-S // MOBA_BLOCK)
    pad = nb * MOBA_BLOCK - S
    nq = S // Q_CHUNK
    top = min(MOBA_TOPK, nb)
    kp = jnp.pad(k, ((0, 0), (0, pad), (0, 0), (0, 0)))
    vp = jnp.pad(v, ((0, 0), (0, pad), (0, 0), (0, 0)))
    kb = kp.reshape(B, nb, MOBA_BLOCK, H, Dh).transpose(0, 3, 1, 2, 4)
    vb = vp.reshape(B, nb, MOBA_BLOCK, H, Dh).transpose(0, 3, 1, 2, 4)
    k_mean = jnp.mean(kb.astype(jnp.float32), axis=3)
    q_blk = jnp.arange(S) // MOBA_BLOCK
    past = jnp.arange(nb)[None, :] < q_blk[:, None]
    gate = jnp.einsum("bshd,bhnd->bhsn", q.astype(jnp.float32), k_mean)
    gate = jnp.where(past[None, None], gate, NEG)
    _, sel = lax.top_k(gate, top)
    q_c = q.reshape(B, nq, Q_CHUNK, H, Dh).transpose(0, 1, 3, 2, 4).reshape(B * nq, H, Q_CHUNK, Dh)
    sel_c = sel.reshape(B, H, nq, Q_CHUNK, top).transpose(0, 2, 1, 3, 4).reshape(B * nq, H, Q_CHUNK, top)
    scale = Dh ** -0.5
    h_idx = jnp.arange(H)[:, None, None]

    def query_block(args):
        i, qc, si = args
        b = i // nq
        c = i % nq
        kb_b = kb[b]
        vb_b = vb[b]
        pos = c * Q_CHUNK + jnp.arange(Q_CHUNK)
        valid = si < (pos // MOBA_BLOCK)[None, :, None]
        k_sel = kb_b[h_idx, si]
        v_sel = vb_b[h_idx, si]
        s_sel = jnp.einsum("hqd,hqnld->hqnl", qc, k_sel).astype(jnp.float32) * scale
        s_sel = jnp.where(valid[..., None], s_sel, NEG).reshape(H, Q_CHUNK, top * MOBA_BLOCK)
        own = (c * Q_CHUNK) // MOBA_BLOCK
        k_own = lax.dynamic_index_in_dim(kb_b, own, axis=1, keepdims=False)
        v_own = lax.dynamic_index_in_dim(vb_b, own, axis=1, keepdims=False)
        s_own = jnp.einsum("hqd,hld->hql", qc, k_own).astype(jnp.float32) * scale
        k_pos = own * MOBA_BLOCK + jnp.arange(MOBA_BLOCK)
        s_own = jnp.where((k_pos[None, :] <= pos[:, None])[None], s_own, NEG)
        p = jax.nn.softmax(jnp.concatenate([s_sel, s_own], axis=-1), axis=-1).astype(qc.dtype)
        p_sel = p[..., :top * MOBA_BLOCK].reshape(H, Q_CHUNK, top, MOBA_BLOCK)
        p_own = p[..., top * MOBA_BLOCK:]
        return (jnp.einsum("hqnl,hqnld->hqd", p_sel, v_sel)
                + jnp.einsum("hql,hld->hqd", p_own, v_own))

    out = lax.map(query_block, (jnp.arange(B * nq), q_c, sel_c))
    return out.reshape(B, nq, H, Q_CHUNK, Dh).transpose(0, 1, 3, 2, 4).reshape(B, S, H * Dh)


def gmlp_spatial_gating(u, v, norm_g, w_s, b_s):
    B, S, _ = u.shape
    nc = S // GMLP_CHUNK
    u = jax.nn.gelu(u)
    v = rmsnorm(jax.nn.gelu(v), norm_g)
    vc = v.reshape(B, nc, GMLP_CHUNK, GMLP_GROUPS, GMLP_GD)
    causal = jnp.tril(jnp.ones((GMLP_CHUNK, GMLP_CHUNK), dtype=bool))
    wm = jnp.where(causal[None], w_s, 0.0).astype(v.dtype)
    z = jnp.einsum("gts,bcsgd->bctgd", wm, vc) + b_s.T.astype(v.dtype)[None, None, :, :, None]
    return u * z.reshape(B, S, GMLP_WIDTH)


def memory_cross_attention(h, m, w_q, w_kv, w_o):
    B, S, _ = h.shape
    M = m.shape[1]
    q = (h @ w_q).reshape(B, S, XA_HEADS, XA_HD)
    k, v = jnp.split(m @ w_kv, 2, axis=-1)
    k = k.reshape(B, M, XA_HEADS, XA_HD)
    v = v.reshape(B, M, XA_HEADS, XA_HD)
    s = jnp.einsum("bshd,bmhd->bhsm", q, k).astype(jnp.float32) * (XA_HD ** -0.5)
    p = jax.nn.softmax(s, axis=-1).astype(h.dtype)
    o = jnp.einsum("bhsm,bmhd->bshd", p, v).reshape(B, S, XA_WIDTH)
    return o @ w_o


def peer_ffn(h, w_q, subkeys, w_down, w_up):
    B, S, D = h.shape
    n_chunks = (B * S) // PEER_TOK_CHUNK
    half = PEER_DQ // 2
    kk = PEER_TOPK * PEER_TOPK

    def token_chunk(xc):
        q = (xc @ w_q).reshape(PEER_TOK_CHUNK, PEER_HEADS, 2, half).astype(jnp.float32)
        s = jnp.einsum("thpd,hpnd->thpn", q, subkeys.astype(jnp.float32))
        s1, i1 = lax.top_k(s[:, :, 0], PEER_TOPK)
        s2, i2 = lax.top_k(s[:, :, 1], PEER_TOPK)
        cand_s = (s1[..., :, None] + s2[..., None, :]).reshape(PEER_TOK_CHUNK, PEER_HEADS, kk)
        cand_i = (i1[..., :, None] * PEER_NKEYS + i2[..., None, :]).reshape(PEER_TOK_CHUNK, PEER_HEADS, kk)
        top_s, top_pos = lax.top_k(cand_s, PEER_TOPK)
        e = jnp.take_along_axis(cand_i, top_pos, axis=-1)
        g = jax.nn.softmax(top_s, axis=-1).astype(xc.dtype)
        a = jax.nn.gelu(jnp.einsum("td,thkd->thk", xc, w_down[e]))
        return jnp.einsum("thk,thkd->td", g * a, w_up[e])

    y = lax.map(token_chunk, h.reshape(n_chunks, PEER_TOK_CHUNK, D))
    return y.reshape(B, S, D)


def setup_inputs(seed: int = 0) -> dict:
    key = jax.random.key(seed)
    ks = jax.random.split(key, 24)

    def nrm(k, shape, scale):
        return jax.random.normal(k, shape, jnp.float32) * scale

    def gain(k, shape):
        return 1.0 + 0.01 * jax.random.normal(k, shape, jnp.float32)

    return {
        "x": nrm(ks[0], (BATCH, SEQ, D_MODEL), 1.0),
        "mem": nrm(ks[1], (BATCH, N_MEM, D_MODEL), 1.0),
        "norm_mix_g": gain(ks[2], (DEPTH, D_MODEL)),
        "w_in": nrm(ks[3], (DEPTH, D_MODEL, IN_COLS), D_MODEL ** -0.5),
        "moba_w_proj": nrm(ks[4], (DEPTH, MOBA_WIDTH, D_MODEL), MOBA_WIDTH ** -0.5),
        "gmlp_norm_g": gain(ks[5], (DEPTH, GMLP_WIDTH)),
        "gmlp_w_s": nrm(ks[6], (DEPTH, GMLP_GROUPS, GMLP_CHUNK, GMLP_CHUNK), GMLP_CHUNK ** -0.5),
        "gmlp_b_s": 1.0 + 0.1 * jax.random.normal(ks[7], (DEPTH, GMLP_GROUPS, GMLP_CHUNK), jnp.float32),
        "gmlp_w_proj": nrm(ks[8], (DEPTH, GMLP_WIDTH, D_MODEL), GMLP_WIDTH ** -0.5),
        "w_out": nrm(ks[9], (DEPTH, D_MODEL, D_MODEL), D_MODEL ** -0.5),
        "norm_xa_g": gain(ks[10], (DEPTH, D_MODEL)),
        "norm_mem_g": gain(ks[11], (DEPTH, D_MODEL)),
        "xa_w_q": nrm(ks[12], (DEPTH, D_MODEL, XA_WIDTH), D_MODEL ** -0.5),
        "xa_w_kv": nrm(ks[13], (DEPTH, D_MODEL, 2 * XA_WIDTH), D_MODEL ** -0.5),
        "xa_w_o": nrm(ks[14], (DEPTH, XA_WIDTH, D_MODEL), XA_WIDTH ** -0.5),
        "norm_ffn_g": gain(ks[15], (DEPTH, D_MODEL)),
        "peer_w_q": nrm(ks[16], (DEPTH, D_MODEL, PEER_HEADS * PEER_DQ), D_MODEL ** -0.5),
        "peer_subkeys": nrm(ks[17], (DEPTH, PEER_HEADS, 2, PEER_NKEYS, PEER_DQ // 2), (PEER_DQ // 2) ** -0.5),
        "peer_w_down": nrm(ks[18], (DEPTH, PEER_N, D_MODEL), D_MODEL ** -0.5),
        "peer_w_up": nrm(ks[19], (DEPTH, PEER_N, D_MODEL), 0.3),
        "final_g": gain(ks[20], (D_MODEL,)),
    }


def reference(x, mem, norm_mix_g, w_in, moba_w_proj, gmlp_norm_g, gmlp_w_s, gmlp_b_s,
              gmlp_w_proj, w_out, norm_xa_g, norm_mem_g, xa_w_q, xa_w_kv, xa_w_o,
              norm_ffn_g, peer_w_q, peer_subkeys, peer_w_down, peer_w_up, final_g):
    B, S, _ = x.shape
    splits = list(np.cumsum([MOBA_WIDTH, MOBA_WIDTH, MOBA_WIDTH, GMLP_WIDTH, GMLP_WIDTH, D_MODEL]))
    for l in range(DEPTH):
        h = rmsnorm(x, norm_mix_g[l])
        proj = h @ w_in[l]
        q, k, v, gu, gv, gate_a, gate_b = jnp.split(proj, splits, axis=-1)
        q = rope(q.reshape(B, S, MOBA_HEADS, MOBA_HD))
        k = rope(k.reshape(B, S, MOBA_HEADS, MOBA_HD))
        v = v.reshape(B, S, MOBA_HEADS, MOBA_HD)
        y_a = moba_attention(q, k, v) @ moba_w_proj[l]
        y_b = gmlp_spatial_gating(gu, gv, gmlp_norm_g[l], gmlp_w_s[l], gmlp_b_s[l]) @ gmlp_w_proj[l]
        merged = jax.nn.sigmoid(gate_a) * y_a + jax.nn.sigmoid(gate_b) * y_b
        x = x + merged @ w_out[l]
        h = rmsnorm(x, norm_xa_g[l])
        m = rmsnorm(mem, norm_mem_g[l])
        x = x + memory_cross_attention(h, m, xa_w_q[l], xa_w_kv[l], xa_w_o[l])
        h = rmsnorm(x, norm_ffn_g[l])
        x = x + peer_ffn(h, peer_w_q[l], peer_subkeys[l], peer_w_down[l], peer_w_up[l])
    return rmsnorm(x, final_g)
```

```python
import functools

import jax
import jax.numpy as jnp
import numpy as np
from jax import lax
from jax.experimental import pallas as pl
from jax.experimental.pallas import tpu as pltpu

F32 = jnp.float32
BF16 = jnp.bfloat16
I32 = jnp.int32
U32 = jnp.uint32

EPS = 1e-6
NEG = -1e30
ROPE_THETA = 10000.0
MOBA_HEADS = 8
MOBA_HD = 64
MOBA_WIDTH = MOBA_HEADS * MOBA_HD
MOBA_BLOCK = 256
MOBA_TOPK = 3
GMLP_GROUPS = 8
GMLP_GD = 64
GMLP_WIDTH = GMLP_GROUPS * GMLP_GD
GMLP_CHUNK = 128
XA_HEADS = 4
XA_HD = 128
XA_WIDTH = XA_HEADS * XA_HD
PEER_HEADS = 8
PEER_NKEYS = 128
PEER_DQ = 256
PEER_TOPK = 16

SUBLANES = 8
LANES = 128
ROW_TILE = MOBA_BLOCK
PEER_TILE = 128
VMEM_LIMIT_BYTES = 56 * 1024 * 1024

_NT = (((1,), (1,)), ((), ()))


def _rmsnorm(x, g):
    ms = jnp.mean(x * x, axis=-1, keepdims=True)
    return x * lax.rsqrt(ms + EPS) * g


def _params(*sem):
    return pltpu.CompilerParams(dimension_semantics=sem, vmem_limit_bytes=VMEM_LIMIT_BYTES)


def _mem_kv_kernel(m_ref, g_ref, w_ref, o_ref):
    h = _rmsnorm(m_ref[...], g_ref[...]).astype(BF16)
    o_ref[...] = jnp.dot(h, w_ref[...], preferred_element_type=F32).astype(BF16)


def _mem_kv(mem2d, g, w_kv, n_mem):
    rows, d = mem2d.shape
    return pl.pallas_call(
        _mem_kv_kernel,
        grid=(rows // n_mem,),
        in_specs=[
            pl.BlockSpec((n_mem, d), lambda i: (i, 0)),
            pl.BlockSpec((1, d), lambda i: (0, 0)),
            pl.BlockSpec(w_kv.shape, lambda i: (0, 0)),
        ],
        out_specs=pl.BlockSpec((n_mem, w_kv.shape[1]), lambda i: (i, 0)),
        out_shape=jax.ShapeDtypeStruct((rows, w_kv.shape[1]), BF16),
        compiler_params=_params("arbitrary"),
        name="mem_kv",
    )(mem2d, g, w_kv)


def _in_proj_kernel(x_ref, g_ref, wtok_ref, wfeat_ref, cos_t_ref, sin_t_ref, cos_f_ref, sin_f_ref,
                    gng_ref, wm_ref, bs_ref,
                    q_ref, k_ref, v_ref, kmean_ref, gm_ref, siga_ref, sigb_ref):
    tb = x_ref.shape[0]
    d = x_ref.shape[1]
    h = _rmsnorm(x_ref[...], g_ref[...]).astype(BF16)
    tok = jnp.dot(h, wtok_ref[...], preferred_element_type=F32)
    feat = lax.dot_general(wfeat_ref[...], h, _NT, preferred_element_type=F32)

    kraw = tok[:, :MOBA_WIDTH]
    lane = lax.broadcasted_iota(I32, kraw.shape, 1)
    half = MOBA_HD // 2
    first_half = (lane % MOBA_HD) < half
    rot = jnp.where(first_half,
                    pltpu.roll(kraw, MOBA_WIDTH - half, axis=1),
                    pltpu.roll(kraw, half, axis=1))
    reps = MOBA_WIDTH // cos_t_ref.shape[1]
    cos_t = jnp.concatenate([cos_t_ref[...]] * reps, axis=1)
    sin_t = jnp.concatenate([sin_t_ref[...]] * reps, axis=1)
    krot = kraw * cos_t + rot * sin_t
    k_ref[...] = krot.astype(BF16)
    kmean_ref[0] = jnp.mean(krot, axis=0, keepdims=True)

    cos_f = cos_f_ref[...]
    sin_f = sin_f_ref[...]
    for hd in range(MOBA_HEADS):
        blk = feat[hd * MOBA_HD:(hd + 1) * MOBA_HD]
        rot_f = jnp.concatenate([blk[half:], blk[:half]], axis=0)
        qrot = blk * cos_f + rot_f * sin_f
        r0 = (hd % 2) * MOBA_HD
        q_ref[0, hd // 2, r0:r0 + MOBA_HD, :] = qrot.astype(BF16)
    for hp in range(MOBA_HEADS // 2):
        r0 = MOBA_WIDTH + hp * 2 * MOBA_HD
        v_ref[0, hp] = feat[r0:r0 + 2 * MOBA_HD].astype(BF16)

    u = jax.nn.gelu(feat[2 * MOBA_WIDTH:2 * MOBA_WIDTH + GMLP_WIDTH])
    v = jax.nn.gelu(feat[2 * MOBA_WIDTH + GMLP_WIDTH:])
    msv = jnp.mean(v * v, axis=0, keepdims=True)
    vn = (v * lax.rsqrt(msv + EPS) * gng_ref[...]).astype(BF16)
    zrows = []
    for g in range(GMLP_GROUPS):
        zc = []
        for c in range(tb // GMLP_CHUNK):
            vb = vn[g * GMLP_GD:(g + 1) * GMLP_GD, c * GMLP_CHUNK:(c + 1) * GMLP_CHUNK]
            z = lax.dot_general(vb, wm_ref[g], _NT, preferred_element_type=F32)
            zc.append(z + bs_ref[g:g + 1, :])
        zrows.append(jnp.concatenate(zc, axis=1))
    z = jnp.concatenate(zrows, axis=0)
    gm_ref[...] = (u * z).T.astype(BF16)

    siga_ref[...] = jax.nn.sigmoid(tok[:, MOBA_WIDTH:MOBA_WIDTH + d]).astype(BF16)
    sigb_ref[...] = jax.nn.sigmoid(tok[:, MOBA_WIDTH + d:]).astype(BF16)


def _in_proj(x2d, g, wtok, wfeat, cos_t, sin_t, cos_f, sin_f, gng, wm, bs, seq):
    t, d = x2d.shape
    tb = ROW_TILE
    nt = t // tb
    nblk = seq // tb
    const2 = lambda i: (0, 0)
    return pl.pallas_call(
        _in_proj_kernel,
        grid=(nt,),
        in_specs=[
            pl.BlockSpec((tb, d), lambda i: (i, 0)),
            pl.BlockSpec((1, d), const2),
            pl.BlockSpec(wtok.shape, const2),
            pl.BlockSpec(wfeat.shape, const2),
            pl.BlockSpec((tb, cos_t.shape[1]), lambda i: (i % nblk, 0)),
            pl.BlockSpec((tb, sin_t.shape[1]), lambda i: (i % nblk, 0)),
            pl.BlockSpec((MOBA_HD, tb), lambda i: (0, i % nblk)),
            pl.BlockSpec((MOBA_HD, tb), lambda i: (0, i % nblk)),
            pl.BlockSpec(gng.shape, const2),
            pl.BlockSpec(wm.shape, lambda i: (0, 0, 0)),
            pl.BlockSpec(bs.shape, const2),
        ],
        out_specs=[
            pl.BlockSpec((1, MOBA_HEADS // 2, 2 * MOBA_HD, tb), lambda i: (i, 0, 0, 0)),
            pl.BlockSpec((tb, MOBA_WIDTH), lambda i: (i, 0)),
            pl.BlockSpec((1, MOBA_HEADS // 2, 2 * MOBA_HD, tb), lambda i: (i, 0, 0, 0)),
            pl.BlockSpec((1, 1, MOBA_WIDTH), lambda i: (i, 0, 0)),
            pl.BlockSpec((tb, GMLP_WIDTH), lambda i: (i, 0)),
            pl.BlockSpec((tb, d), lambda i: (i, 0)),
            pl.BlockSpec((tb, d), lambda i: (i, 0)),
        ],
        out_shape=[
            jax.ShapeDtypeStruct((nt, MOBA_HEADS // 2, 2 * MOBA_HD, tb), BF16),
            jax.ShapeDtypeStruct((t, MOBA_WIDTH), BF16),
            jax.ShapeDtypeStruct((nt, MOBA_HEADS // 2, 2 * MOBA_HD, tb), BF16),
            jax.ShapeDtypeStruct((nt, 1, MOBA_WIDTH), F32),
            jax.ShapeDtypeStruct((t, GMLP_WIDTH), BF16),
            jax.ShapeDtypeStruct((t, d), BF16),
            jax.ShapeDtypeStruct((t, d), BF16),
        ],
        compiler_params=_params("arbitrary"),
        name="in_proj",
    )(x2d, g, wtok, wfeat, cos_t, sin_t, cos_f, sin_f, gng, wm, bs)


def _moba_kernel(q_ref, k_ref, v_ref, km_ref, o_ref, bias_ref, acc_ref, m_ref, l_ref):
    blk = MOBA_BLOCK
    i = pl.program_id(2)
    nb = km_ref.shape[1]
    scale = MOBA_HD ** -0.5
    q = (q_ref[0, 0].astype(F32) * scale).astype(BF16)
    zero = jnp.zeros((MOBA_HD, blk), BF16)
    qpad = [jnp.concatenate([q[:MOBA_HD], zero], axis=0),
            jnp.concatenate([zero, q[MOBA_HD:]], axis=0)]

    km = km_ref[0]
    km_hi = km.astype(BF16)
    km_lo = (km - km_hi.astype(F32)).astype(BF16)
    row = lax.broadcasted_iota(I32, (nb, blk), 0)
    past = row < i
    for a in range(2):
        gate = (jnp.dot(km_hi, qpad[a], preferred_element_type=F32)
                + jnp.dot(km_lo, qpad[a], preferred_element_type=F32))
        g = jnp.where(past, gate, NEG)
        sel = jnp.zeros((nb, blk), jnp.bool_)
        for _ in range(MOBA_TOPK):
            mx = jnp.max(g, axis=0, keepdims=True)
            first = jnp.min(jnp.where(g == mx, row, nb), axis=0, keepdims=True)
            pick = (row == first) & (mx > NEG)
            sel = sel | pick
            g = jnp.where(pick, NEG, g)
        bias_ref[a] = jnp.where(sel, 0.0, NEG)

    kb = k_ref[pl.ds(pl.multiple_of(i * blk, blk), blk), :]
    vb = v_ref[i, 0]
    kpos = lax.broadcasted_iota(I32, (blk, blk), 0)
    qpos = lax.broadcasted_iota(I32, (blk, blk), 1)
    causal = kpos <= qpos
    for a in range(2):
        s = jnp.dot(kb, qpad[a], preferred_element_type=F32)
        s = jnp.where(causal, s, NEG)
        mx = jnp.max(s, axis=0, keepdims=True)
        p = jnp.exp(s - mx)
        m_ref[a] = mx
        l_ref[a] = jnp.sum(p, axis=0, keepdims=True)
        acc_ref[a] = jnp.dot(vb[a * MOBA_HD:(a + 1) * MOBA_HD], p.astype(BF16),
                             preferred_element_type=F32)

    def past_block(j, carry):
        kj = k_ref[pl.ds(pl.multiple_of(j * blk, blk), blk), :]
        vj = v_ref[j, 0]
        for a in range(2):
            s = jnp.dot(kj, qpad[a], preferred_element_type=F32) + bias_ref[a, pl.ds(j, 1), :]
            m_old = m_ref[a]
            m_new = jnp.maximum(m_old, jnp.max(s, axis=0, keepdims=True))
            alpha = jnp.exp(m_old - m_new)
            p = jnp.exp(s - m_new)
            l_ref[a] = alpha * l_ref[a] + jnp.sum(p, axis=0, keepdims=True)
            acc_ref[a] = alpha * acc_ref[a] + jnp.dot(vj[a * MOBA_HD:(a + 1) * MOBA_HD], p.astype(BF16),
                                                      preferred_element_type=F32)
            m_ref[a] = m_new
        return carry

    lax.fori_loop(0, i, past_block, 0)
    out = jnp.concatenate([acc_ref[0] / l_ref[0], acc_ref[1] / l_ref[1]], axis=0)
    o_ref[...] = out.T.astype(BF16)


def _moba(q4, k2d, v4, kmean, batch, seq):
    blk = MOBA_BLOCK
    nb = seq // blk
    hp = MOBA_HEADS // 2
    t = k2d.shape[0]
    return pl.pallas_call(
        _moba_kernel,
        grid=(batch, hp, nb),
        in_specs=[
            pl.BlockSpec((1, 1, 2 * MOBA_HD, blk), lambda b, p, i: (b * nb + i, p, 0, 0)),
            pl.BlockSpec((seq, 2 * MOBA_HD), lambda b, p, i: (b, p)),
            pl.BlockSpec((nb, 1, 2 * MOBA_HD, blk), lambda b, p, i: (b, p, 0, 0)),
            pl.BlockSpec((1, nb, 2 * MOBA_HD), lambda b, p, i: (b, 0, p)),
        ],
        out_specs=pl.BlockSpec((blk, 2 * MOBA_HD), lambda b, p, i: (b * nb + i, p)),
        out_shape=jax.ShapeDtypeStruct((t, MOBA_WIDTH), BF16),
        scratch_shapes=[
            pltpu.VMEM((2, nb, blk), F32),
            pltpu.VMEM((2, MOBA_HD, blk), F32),
            pltpu.VMEM((2, 1, blk), F32),
            pltpu.VMEM((2, 1, blk), F32),
        ],
        compiler_params=_params("arbitrary", "arbitrary", "arbitrary"),
        name="moba",
    )(q4, k2d, v4, kmean)


def _extract_topk(s, payload, k, vals_ref, pay_ref, base):
    r = s.shape[0]
    row = lax.broadcasted_iota(I32, s.shape, 0)
    for j in range(k):
        mx = jnp.max(s, axis=0, keepdims=True)
        first = jnp.min(jnp.where(s == mx, row, r), axis=0, keepdims=True)
        pick = row == first
        vals_ref[base + j:base + j + 1, :] = mx
        pay_ref[base + j:base + j + 1, :] = jnp.sum(jnp.where(pick, payload, 0), axis=0, keepdims=True)
        s = jnp.where(pick, -jnp.inf, s)


def _mix_kernel(x_ref, att_ref, gm_ref, siga_ref, sigb_ref, kv_ref,
                wpa_ref, wpb_ref, wout_ref, gxa_ref, wxq_ref, wxo_ref, gffn_ref, wpq_ref, sk_ref,
                x2_ref, h3_ref, prow_ref, sh_ref, gate_ref,
                s1v_ref, s1i_ref, s2v_ref, s2i_ref, tv_ref, te_ref):
    tb = x_ref.shape[0]
    ya = jnp.dot(att_ref[...], wpa_ref[...], preferred_element_type=F32)
    yb = jnp.dot(gm_ref[...], wpb_ref[...], preferred_element_type=F32)
    merged = siga_ref[...].astype(F32) * ya + sigb_ref[...].astype(F32) * yb
    x1 = x_ref[...] + jnp.dot(merged.astype(BF16), wout_ref[...], preferred_element_type=F32)

    h2 = _rmsnorm(x1, gxa_ref[...]).astype(BF16)
    qx = jnp.dot(h2, wxq_ref[...], preferred_element_type=F32).astype(BF16)
    kv = kv_ref[...]
    heads = []
    for hd in range(XA_HEADS):
        qh = qx[:, hd * XA_HD:(hd + 1) * XA_HD]
        kh = kv[:, hd * XA_HD:(hd + 1) * XA_HD]
        vh = kv[:, XA_WIDTH + hd * XA_HD:XA_WIDTH + (hd + 1) * XA_HD]
        s = lax.dot_general(qh, kh, _NT, preferred_element_type=F32) * (XA_HD ** -0.5)
        mx = jnp.max(s, axis=-1, keepdims=True)
        p = jnp.exp(s - mx)
        p = p / jnp.sum(p, axis=-1, keepdims=True)
        heads.append(jnp.dot(p.astype(BF16), vh, preferred_element_type=F32))
    o = jnp.concatenate(heads, axis=1).astype(BF16)
    x2 = x1 + jnp.dot(o, wxo_ref[...], preferred_element_type=F32)
    x2_ref[...] = x2

    h3 = _rmsnorm(x2, gffn_ref[...])
    h3_ref[...] = h3
    pq = jnp.dot(h3.astype(BF16), wpq_ref[...], preferred_element_type=F32).astype(BF16)
    half = PEER_DQ // 2
    kk = PEER_TOPK
    key_row = lax.broadcasted_iota(I32, (PEER_NKEYS, tb), 0)
    for hd in range(PEER_HEADS):
        for part, (vref, iref) in enumerate(((s1v_ref, s1i_ref), (s2v_ref, s2i_ref))):
            c0 = (hd * 2 + part) * half
            st = lax.dot_general(sk_ref[hd * 2 + part], pq[:, c0:c0 + half], _NT,
                                 preferred_element_type=F32)
            _extract_topk(st, key_row, kk, vref, iref, 0)
        s1v, s1i, s2v, s2i = s1v_ref[...], s1i_ref[...], s2v_ref[...], s2i_ref[...]
        blocks_s, blocks_e = [], []
        i = 0
        while i < kk:
            n = kk // (i + 1)
            if n == 1:
                cnt = kk - i
                blocks_s.append(s1v[i:i + cnt] + s2v[0:1])
                blocks_e.append(s1i[i:i + cnt] * PEER_NKEYS + s2i[0:1])
                i += cnt
                continue
            rows = -(-n // SUBLANES) * SUBLANES
            cs = s1v[i:i + 1] + s2v[0:rows]
            ce = s1i[i:i + 1] * PEER_NKEYS + s2i[0:rows]
            if rows != n:
                keep = lax.broadcasted_iota(I32, (rows, tb), 0) < n
                cs = jnp.where(keep, cs, -jnp.inf)
            blocks_s.append(cs)
            blocks_e.append(ce)
            i += 1
        cand_s = jnp.concatenate(blocks_s, axis=0)
        cand_e = jnp.concatenate(blocks_e, axis=0)
        _extract_topk(cand_s, cand_e, kk, tv_ref, te_ref, hd * kk)

        tv = tv_ref[hd * kk:(hd + 1) * kk, :]
        ev = te_ref[hd * kk:(hd + 1) * kk, :]
        pe = jnp.exp(tv - jnp.max(tv, axis=0, keepdims=True))
        gate_ref[hd * kk:(hd + 1) * kk, :] = pe / jnp.sum(pe, axis=0, keepdims=True)
        prow_ref[hd * kk:(hd + 1) * kk, :] = ev >> 1
        sh_ref[hd * kk:(hd + 1) * kk, :] = (1 - (ev & 1)) * 16


def _mix(x2d, att, gm, siga, sigb, kv, wpa, wpb, wout, gxa, wxq, wxo, gffn, wpq, sk, seq, n_mem):
    t, d = x2d.shape
    tb = ROW_TILE
    nt = t // tb
    per_b = seq // tb
    const2 = lambda i: (0, 0)
    rowspec = lambda w: pl.BlockSpec((tb, w), lambda i: (i, 0))
    colspec = pl.BlockSpec((PEER_HEADS * PEER_TOPK, tb), lambda i: (0, i))
    nsel = PEER_HEADS * PEER_TOPK
    return pl.pallas_call(
        _mix_kernel,
        grid=(nt,),
        in_specs=[
            rowspec(d), rowspec(MOBA_WIDTH), rowspec(GMLP_WIDTH), rowspec(d), rowspec(d),
            pl.BlockSpec((n_mem, kv.shape[1]), lambda i: (i // per_b, 0)),
            pl.BlockSpec(wpa.shape, const2), pl.BlockSpec(wpb.shape, const2),
            pl.BlockSpec(wout.shape, const2), pl.BlockSpec((1, d), const2),
            pl.BlockSpec(wxq.shape, const2), pl.BlockSpec(wxo.shape, const2),
            pl.BlockSpec((1, d), const2), pl.BlockSpec(wpq.shape, const2),
            pl.BlockSpec(sk.shape, lambda i: (0, 0, 0)),
        ],
        out_specs=[rowspec(d), rowspec(d), colspec, colspec, colspec],
        out_shape=[
            jax.ShapeDtypeStruct((t, d), F32),
            jax.ShapeDtypeStruct((t, d), F32),
            jax.ShapeDtypeStruct((nsel, t), I32),
            jax.ShapeDtypeStruct((nsel, t), I32),
            jax.ShapeDtypeStruct((nsel, t), F32),
        ],
        scratch_shapes=[
            pltpu.VMEM((PEER_TOPK, tb), F32), pltpu.VMEM((PEER_TOPK, tb), I32),
            pltpu.VMEM((PEER_TOPK, tb), F32), pltpu.VMEM((PEER_TOPK, tb), I32),
            pltpu.VMEM((nsel, tb), F32), pltpu.VMEM((nsel, tb), I32),
        ],
        compiler_params=_params("arbitrary"),
        name="mix_xattn",
    )(x2d, att, gm, siga, sigb, kv, wpa, wpb, wout, gxa, wxq, wxo, gffn, wpq, sk)


def _unpack_row(tbl_ref, prow, sh):
    w = tbl_ref[prow]
    return pltpu.bitcast((w << sh.astype(U32)) & jnp.uint32(0xFFFF0000), F32)


def _sublane_sums8(p):
    sub = lax.broadcasted_iota(I32, (SUBLANES, LANES), 0)

    def comb(x, y, s):
        lo = (sub & s) == 0
        return (jnp.where(lo, x, pltpu.roll(y, s, axis=0))
                + jnp.where(lo, pltpu.roll(x, SUBLANES - s, axis=0), y))

    z = [comb(p[0], p[4], 4), comb(p[2], p[6], 4), comb(p[1], p[5], 4), comb(p[3], p[7], 4)]
    w = [comb(z[0], z[1], 2), comb(z[2], z[3], 2)]
    return comb(w[0], w[1], 1)


def _peer_down_kernel(prow_ref, sh_ref, h_ref, gate_ref, tbl_ref, c_ref, acc_ref):
    tt = h_ref.shape[0]
    nsel = prow_ref.shape[1]
    lane_id = lax.broadcasted_iota(I32, (nsel, tt), 1)

    def token(t, carry):
        h = h_ref[t]
        cols = []
        for grp in range(nsel // SUBLANES):
            prods = []
            for j in range(SUBLANES):
                k = grp * SUBLANES + j
                prods.append(_unpack_row(tbl_ref, prow_ref[t, k], sh_ref[t, k]) * h)
            cols.append(jnp.sum(_sublane_sums8(prods), axis=1, keepdims=True))
        col = jnp.concatenate(cols, axis=0)
        acc_ref[...] = jnp.where(lane_id == t, col, acc_ref[...])
        return carry

    lax.fori_loop(0, tt, token, 0)
    c_ref[...] = gate_ref[...] * jax.nn.gelu(acc_ref[...])


def _peer_down(prow, sh, h3, gate_t, tbl):
    t, nsel = prow.shape
    tt = PEER_TILE
    smem = lambda: pl.BlockSpec((tt, nsel), lambda i: (i, 0), memory_space=pltpu.SMEM)
    return pl.pallas_call(
        _peer_down_kernel,
        grid=(t // tt,),
        in_specs=[
            smem(), smem(),
            pl.BlockSpec((tt, SUBLANES, LANES), lambda i: (i, 0, 0)),
            pl.BlockSpec((nsel, tt), lambda i: (0, i)),
            pl.BlockSpec(tbl.shape, lambda i: (0, 0, 0), pipeline_mode=pl.Buffered(1)),
        ],
        out_specs=pl.BlockSpec((nsel, tt), lambda i: (0, i)),
        out_shape=jax.ShapeDtypeStruct((nsel, t), F32),
        scratch_shapes=[pltpu.VMEM((nsel, tt), F32)],
        compiler_params=_params("arbitrary"),
        name="peer_down",
    )(prow, sh, h3, gate_t, tbl)


def _peer_up_kernel(prow_ref, sh_ref, c_ref, x_ref, tbl_ref, o_ref):
    tt = x_ref.shape[0]
    nsel = prow_ref.shape[1]
    nacc = 4

    def token(t, carry):
        acc = [jnp.zeros((SUBLANES, LANES), F32) for _ in range(nacc)]
        for k in range(nsel):
            w = _unpack_row(tbl_ref, prow_ref[t, k], sh_ref[t, k])
            acc[k % nacc] = acc[k % nacc] + w * c_ref[t, k]
        o_ref[t] = x_ref[t] + ((acc[0] + acc[1]) + (acc[2] + acc[3]))
        return carry

    lax.fori_loop(0, tt, token, 0)


def _peer_up(prow, sh, c, x3d, tbl):
    t, nsel = prow.shape
    tt = PEER_TILE
    smem = lambda: pl.BlockSpec((tt, nsel), lambda i: (i, 0), memory_space=pltpu.SMEM)
    return pl.pallas_call(
        _peer_up_kernel,
        grid=(t // tt,),
        in_specs=[
            smem(), smem(), smem(),
            pl.BlockSpec((tt, SUBLANES, LANES), lambda i: (i, 0, 0)),
            pl.BlockSpec(tbl.shape, lambda i: (0, 0, 0), pipeline_mode=pl.Buffered(1)),
        ],
        out_specs=pl.BlockSpec((tt, SUBLANES, LANES), lambda i: (i, 0, 0)),
        out_shape=jax.ShapeDtypeStruct(x3d.shape, F32),
        compiler_params=_params("arbitrary"),
        name="peer_up",
    )(prow, sh, c, x3d, tbl)


def _final_norm_kernel(x_ref, g_ref, o_ref):
    o_ref[...] = _rmsnorm(x_ref[...], g_ref[...])


def _final_norm(x2d, g):
    t, d = x2d.shape
    tb = 2 * ROW_TILE
    return pl.pallas_call(
        _final_norm_kernel,
        grid=(t // tb,),
        in_specs=[pl.BlockSpec((tb, d), lambda i: (i, 0)), pl.BlockSpec((1, d), lambda i: (0, 0))],
        out_specs=pl.BlockSpec((tb, d), lambda i: (i, 0)),
        out_shape=jax.ShapeDtypeStruct((t, d), F32),
        compiler_params=_params("arbitrary"),
        name="final_norm",
    )(x2d, g)


def _pack_expert_table(w):
    n, d = w.shape
    bits = lax.bitcast_convert_type(w.astype(BF16), jnp.uint16).astype(U32).reshape(n // 2, 2, d)
    packed = bits[:, 0, :] | (bits[:, 1, :] << 16)
    return packed.reshape(n // 2, SUBLANES, d // SUBLANES)


def _rope_tables(seq):
    half = MOBA_HD // 2
    freqs = ROPE_THETA ** (-jnp.arange(half, dtype=F32) / half)
    ang = jnp.arange(seq, dtype=F32)[:, None] * freqs[None, :]
    cos = jnp.cos(ang)
    sin = jnp.sin(ang)
    cos_h = jnp.concatenate([cos, cos], axis=1)
    sin_h = jnp.concatenate([-sin, sin], axis=1)
    lane_reps = LANES // MOBA_HD
    return (jnp.tile(cos_h, (1, lane_reps)), jnp.tile(sin_h, (1, lane_reps)), cos_h.T, sin_h.T)


def _layer(x2d, kv, batch, seq, n_mem, norm_mix_g, w_in, moba_w_proj, gmlp_norm_g, gmlp_w_s, gmlp_b_s,
           gmlp_w_proj, w_out, norm_xa_g, xa_w_q, xa_w_o, norm_ffn_g, peer_w_q, peer_subkeys,
           peer_w_down, peer_w_up):
    t, d = x2d.shape
    c_q, c_k, c_v = 0, MOBA_WIDTH, 2 * MOBA_WIDTH
    c_gu = 3 * MOBA_WIDTH
    c_gv = c_gu + GMLP_WIDTH
    c_ga = c_gv + GMLP_WIDTH
    wtok = jnp.concatenate([w_in[:, c_k:c_v], w_in[:, c_ga:]], axis=1).astype(BF16)
    wfeat = jnp.concatenate([w_in[:, c_q:c_k], w_in[:, c_v:c_gu], w_in[:, c_gu:c_ga]], axis=1).T.astype(BF16)
    cos_t, sin_t, cos_f, sin_f = _rope_tables(seq)
    causal = jnp.tril(jnp.ones((GMLP_CHUNK, GMLP_CHUNK), dtype=bool))
    wm = jnp.where(causal[None], gmlp_w_s, 0.0).astype(BF16)
    gng = jnp.broadcast_to(gmlp_norm_g[:, None], (GMLP_WIDTH, ROW_TILE))

    q4, k2d, v4, kmean, gm, siga, sigb = _in_proj(
        x2d, norm_mix_g[None, :], wtok, wfeat, cos_t, sin_t, cos_f, sin_f, gng, wm, gmlp_b_s, seq)
    att = _moba(q4, k2d, v4, kmean.reshape(batch, seq // MOBA_BLOCK, MOBA_WIDTH), batch, seq)

    sk = peer_subkeys.reshape(PEER_HEADS * 2, PEER_NKEYS, PEER_DQ // 2).astype(BF16)
    x2, h3, prow_t, sh_t, gate_t = _mix(
        x2d, att, gm, siga, sigb, kv,
        moba_w_proj.astype(BF16), gmlp_w_proj.astype(BF16), w_out.astype(BF16), norm_xa_g[None, :],
        xa_w_q.astype(BF16), xa_w_o.astype(BF16), norm_ffn_g[None, :], peer_w_q.astype(BF16), sk,
        seq, n_mem)

    prow = prow_t.T
    sh = sh_t.T
    c_t = _peer_down(prow, sh, h3.reshape(t, SUBLANES, d // SUBLANES), gate_t, _pack_expert_table(peer_w_down))
    x3 = _peer_up(prow, sh, c_t.T, x2.reshape(t, SUBLANES, d // SUBLANES), _pack_expert_table(peer_w_up))
    return x3.reshape(t, d)


def kernel(x, mem, norm_mix_g, w_in, moba_w_proj, gmlp_norm_g, gmlp_w_s, gmlp_b_s, gmlp_w_proj, w_out,
           norm_xa_g, norm_mem_g, xa_w_q, xa_w_kv, xa_w_o, norm_ffn_g, peer_w_q, peer_subkeys,
           peer_w_down, peer_w_up, final_g):
    batch, seq, d = x.shape
    n_mem = mem.shape[1]
    depth = w_in.shape[0]
    assert seq % MOBA_BLOCK == 0 and (batch * seq) % PEER_TILE == 0 and d == SUBLANES * LANES
    x2d = x.reshape(batch * seq, d)
    mem2d = mem.reshape(batch * n_mem, d)
    for l in range(depth):
        kv = _mem_kv(mem2d, norm_mem_g[l][None, :], xa_w_kv[l].astype(BF16), n_mem)
        x2d = _layer(x2d, kv, batch, seq, n_mem, norm_mix_g[l], w_in[l], moba_w_proj[l], gmlp_norm_g[l],
                     gmlp_w_s[l], gmlp_b_s[l], gmlp_w_proj[l], w_out[l], norm_xa_g[l], xa_w_q[l],
                     xa_w_o[l], norm_ffn_g[l], peer_w_q[l], peer_subkeys[l], peer_w_down[l], peer_w_up[l])
    return _final_norm(x2d, final_g[None, :]).reshape(batch, seq, d)
```

```python
import functools

import jax
import jax.numpy as jnp
import numpy as np
from jax import lax
from jax.experimental import pallas as pl
from jax.experimental.pallas import tpu as pltpu

F32 = jnp.float32
BF16 = jnp.bfloat16
I32 = jnp.int32
U32 = jnp.uint32

EPS = 1e-6
NEG = -1e30
ROPE_THETA = 10000.0
MOBA_HEADS = 8
MOBA_HD = 64
MOBA_WIDTH = MOBA_HEADS * MOBA_HD
MOBA_BLOCK = 256
MOBA_TOPK = 3
GMLP_GROUPS = 8
GMLP_GD = 64
GMLP_WIDTH = GMLP_GROUPS * GMLP_GD
GMLP_CHUNK = 128
XA_HEADS = 4
XA_HD = 128
XA_WIDTH = XA_HEADS * XA_HD
PEER_HEADS = 8
PEER_NKEYS = 128
PEER_DQ = 256
PEER_TOPK = 16

SUBLANES = 8
LANES = 128
ROW_TILE = MOBA_BLOCK
PEER_TILE = 128
PEER_CHUNK = SUBLANES
VMEM_LIMIT_BYTES = 56 * 1024 * 1024

_NT = (((1,), (1,)), ((), ()))


def _rmsnorm(x, g):
    ms = jnp.mean(x * x, axis=-1, keepdims=True)
    return x * lax.rsqrt(ms + EPS) * g


def _params(*sem):
    return pltpu.CompilerParams(dimension_semantics=sem, vmem_limit_bytes=VMEM_LIMIT_BYTES)


def _mem_kv_kernel(m_ref, g_ref, w_ref, o_ref):
    h = _rmsnorm(m_ref[...], g_ref[...]).astype(BF16)
    o_ref[...] = jnp.dot(h, w_ref[...], preferred_element_type=F32).astype(BF16)


def _mem_kv(mem2d, g, w_kv, n_mem):
    rows, d = mem2d.shape
    return pl.pallas_call(
        _mem_kv_kernel,
        grid=(rows // n_mem,),
        in_specs=[
            pl.BlockSpec((n_mem, d), lambda i: (i, 0)),
            pl.BlockSpec((1, d), lambda i: (0, 0)),
            pl.BlockSpec(w_kv.shape, lambda i: (0, 0)),
        ],
        out_specs=pl.BlockSpec((n_mem, w_kv.shape[1]), lambda i: (i, 0)),
        out_shape=jax.ShapeDtypeStruct((rows, w_kv.shape[1]), BF16),
        compiler_params=_params("arbitrary"),
        name="mem_kv",
    )(mem2d, g, w_kv)


def _in_proj_kernel(x_ref, g_ref, wtok_ref, wfeat_ref, cos_t_ref, sin_t_ref, cos_f_ref, sin_f_ref,
                    gng_ref, wm_ref, bs_ref,
                    q_ref, k_ref, v_ref, kmean_ref, gm_ref, siga_ref, sigb_ref):
    tb = x_ref.shape[0]
    d = x_ref.shape[1]
    h = _rmsnorm(x_ref[...], g_ref[...]).astype(BF16)
    tok = jnp.dot(h, wtok_ref[...], preferred_element_type=F32)
    feat = lax.dot_general(wfeat_ref[...], h, _NT, preferred_element_type=F32)

    kraw = tok[:, :MOBA_WIDTH]
    lane = lax.broadcasted_iota(I32, kraw.shape, 1)
    half = MOBA_HD // 2
    first_half = (lane % MOBA_HD) < half
    rot = jnp.where(first_half,
                    pltpu.roll(kraw, MOBA_WIDTH - half, axis=1),
                    pltpu.roll(kraw, half, axis=1))
    reps = MOBA_WIDTH // cos_t_ref.shape[1]
    cos_t = jnp.concatenate([cos_t_ref[...]] * reps, axis=1)
    sin_t = jnp.concatenate([sin_t_ref[...]] * reps, axis=1)
    krot = kraw * cos_t + rot * sin_t
    k_ref[...] = krot.astype(BF16)
    kmean_ref[0] = jnp.mean(krot, axis=0, keepdims=True)

    cos_f = cos_f_ref[...]
    sin_f = sin_f_ref[...]
    for hd in range(MOBA_HEADS):
        blk = feat[hd * MOBA_HD:(hd + 1) * MOBA_HD]
        rot_f = jnp.concatenate([blk[half:], blk[:half]], axis=0)
        qrot = blk * cos_f + rot_f * sin_f
        r0 = (hd % 2) * MOBA_HD
        q_ref[0, hd // 2, r0:r0 + MOBA_HD, :] = qrot.astype(BF16)
    for hp in range(MOBA_HEADS // 2):
        r0 = MOBA_WIDTH + hp * 2 * MOBA_HD
        v_ref[0, hp] = feat[r0:r0 + 2 * MOBA_HD].astype(BF16)

    u = jax.nn.gelu(feat[2 * MOBA_WIDTH:2 * MOBA_WIDTH + GMLP_WIDTH])
    v = jax.nn.gelu(feat[2 * MOBA_WIDTH + GMLP_WIDTH:])
    msv = jnp.mean(v * v, axis=0, keepdims=True)
    vn = (v * lax.rsqrt(msv + EPS) * gng_ref[...]).astype(BF16)
    zrows = []
    for g in range(GMLP_GROUPS):
        zc = []
        for c in range(tb // GMLP_CHUNK):
            vb = vn[g * GMLP_GD:(g + 1) * GMLP_GD, c * GMLP_CHUNK:(c + 1) * GMLP_CHUNK]
            z = lax.dot_general(vb, wm_ref[g], _NT, preferred_element_type=F32)
            zc.append(z + bs_ref[g:g + 1, :])
        zrows.append(jnp.concatenate(zc, axis=1))
    z = jnp.concatenate(zrows, axis=0)
    gm_ref[...] = (u * z).T.astype(BF16)

    siga_ref[...] = jax.nn.sigmoid(tok[:, MOBA_WIDTH:MOBA_WIDTH + d]).astype(BF16)
    sigb_ref[...] = jax.nn.sigmoid(tok[:, MOBA_WIDTH + d:]).astype(BF16)


def _in_proj(x2d, g, wtok, wfeat, cos_t, sin_t, cos_f, sin_f, gng, wm, bs, seq):
    t, d = x2d.shape
    tb = ROW_TILE
    nt = t // tb
    nblk = seq // tb
    const2 = lambda i: (0, 0)
    return pl.pallas_call(
        _in_proj_kernel,
        grid=(nt,),
        in_specs=[
            pl.BlockSpec((tb, d), lambda i: (i, 0)),
            pl.BlockSpec((1, d), const2),
            pl.BlockSpec(wtok.shape, const2),
            pl.BlockSpec(wfeat.shape, const2),
            pl.BlockSpec((tb, cos_t.shape[1]), lambda i: (i % nblk, 0)),
            pl.BlockSpec((tb, sin_t.shape[1]), lambda i: (i % nblk, 0)),
            pl.BlockSpec((MOBA_HD, tb), lambda i: (0, i % nblk)),
            pl.BlockSpec((MOBA_HD, tb), lambda i: (0, i % nblk)),
            pl.BlockSpec(gng.shape, const2),
            pl.BlockSpec(wm.shape, lambda i: (0, 0, 0)),
            pl.BlockSpec(bs.shape, const2),
        ],
        out_specs=[
            pl.BlockSpec((1, MOBA_HEADS // 2, 2 * MOBA_HD, tb), lambda i: (i, 0, 0, 0)),
            pl.BlockSpec((tb, MOBA_WIDTH), lambda i: (i, 0)),
            pl.BlockSpec((1, MOBA_HEADS // 2, 2 * MOBA_HD, tb), lambda i: (i, 0, 0, 0)),
            pl.BlockSpec((1, 1, MOBA_WIDTH), lambda i: (i, 0, 0)),
            pl.BlockSpec((tb, GMLP_WIDTH), lambda i: (i, 0)),
            pl.BlockSpec((tb, d), lambda i: (i, 0)),
            pl.BlockSpec((tb, d), lambda i: (i, 0)),
        ],
        out_shape=[
            jax.ShapeDtypeStruct((nt, MOBA_HEADS // 2, 2 * MOBA_HD, tb), BF16),
            jax.ShapeDtypeStruct((t, MOBA_WIDTH), BF16),
            jax.ShapeDtypeStruct((nt, MOBA_HEADS // 2, 2 * MOBA_HD, tb), BF16),
            jax.ShapeDtypeStruct((nt, 1, MOBA_WIDTH), F32),
            jax.ShapeDtypeStruct((t, GMLP_WIDTH), BF16),
            jax.ShapeDtypeStruct((t, d), BF16),
            jax.ShapeDtypeStruct((t, d), BF16),
        ],
        compiler_params=_params("arbitrary"),
        name="in_proj",
    )(x2d, g, wtok, wfeat, cos_t, sin_t, cos_f, sin_f, gng, wm, bs)


def _moba_kernel(q_ref, k_ref, v_ref, km_ref, o_ref, bias_ref, acc_ref, m_ref, l_ref):
    blk = MOBA_BLOCK
    i = pl.program_id(2)
    nb = km_ref.shape[1]
    scale = MOBA_HD ** -0.5
    q = (q_ref[0, 0].astype(F32) * scale).astype(BF16)
    zero = jnp.zeros((MOBA_HD, blk), BF16)
    qpad = [jnp.concatenate([q[:MOBA_HD], zero], axis=0),
            jnp.concatenate([zero, q[MOBA_HD:]], axis=0)]

    km = km_ref[0]
    km_hi = km.astype(BF16)
    km_lo = (km - km_hi.astype(F32)).astype(BF16)
    row = lax.broadcasted_iota(I32, (nb, blk), 0)
    past = row < i
    for a in range(2):
        gate = (jnp.dot(km_hi, qpad[a], preferred_element_type=F32)
                + jnp.dot(km_lo, qpad[a], preferred_element_type=F32))
        g = jnp.where(past, gate, NEG)
        sel = jnp.zeros((nb, blk), jnp.bool_)
        for _ in range(MOBA_TOPK):
            mx = jnp.max(g, axis=0, keepdims=True)
            first = jnp.min(jnp.where(g == mx, row, nb), axis=0, keepdims=True)
            pick = (row == first) & (mx > NEG)
            sel = sel | pick
            g = jnp.where(pick, NEG, g)
        bias_ref[a] = jnp.where(sel, 0.0, NEG)

    kb = k_ref[pl.ds(pl.multiple_of(i * blk, blk), blk), :]
    vb = v_ref[i, 0]
    kpos = lax.broadcasted_iota(I32, (blk, blk), 0)
    qpos = lax.broadcasted_iota(I32, (blk, blk), 1)
    causal = kpos <= qpos
    for a in range(2):
        s = jnp.dot(kb, qpad[a], preferred_element_type=F32)
        s = jnp.where(causal, s, NEG)
        mx = jnp.max(s, axis=0, keepdims=True)
        p = jnp.exp(s - mx)
        m_ref[a] = mx
        l_ref[a] = jnp.sum(p, axis=0, keepdims=True)
        acc_ref[a] = jnp.dot(vb[a * MOBA_HD:(a + 1) * MOBA_HD], p.astype(BF16),
                             preferred_element_type=F32)

    def past_block(j, carry):
        kj = k_ref[pl.ds(pl.multiple_of(j * blk, blk), blk), :]
        vj = v_ref[j, 0]
        for a in range(2):
            s = jnp.dot(kj, qpad[a], preferred_element_type=F32) + bias_ref[a, pl.ds(j, 1), :]
            m_old = m_ref[a]
            m_new = jnp.maximum(m_old, jnp.max(s, axis=0, keepdims=True))
            alpha = jnp.exp(m_old - m_new)
            p = jnp.exp(s - m_new)
            l_ref[a] = alpha * l_ref[a] + jnp.sum(p, axis=0, keepdims=True)
            acc_ref[a] = alpha * acc_ref[a] + jnp.dot(vj[a * MOBA_HD:(a + 1) * MOBA_HD], p.astype(BF16),
                                                      preferred_element_type=F32)
            m_ref[a] = m_new
        return carry

    lax.fori_loop(0, i, past_block, 0)
    out = jnp.concatenate([acc_ref[0] / l_ref[0], acc_ref[1] / l_ref[1]], axis=0)
    o_ref[...] = out.T.astype(BF16)


def _moba(q4, k2d, v4, kmean, batch, seq):
    blk = MOBA_BLOCK
    nb = seq // blk
    hp = MOBA_HEADS // 2
    t = k2d.shape[0]
    return pl.pallas_call(
        _moba_kernel,
        grid=(batch, hp, nb),
        in_specs=[
            pl.BlockSpec((1, 1, 2 * MOBA_HD, blk), lambda b, p, i: (b * nb + i, p, 0, 0)),
            pl.BlockSpec((seq, 2 * MOBA_HD), lambda b, p, i: (b, p)),
            pl.BlockSpec((nb, 1, 2 * MOBA_HD, blk), lambda b, p, i: (b, p, 0, 0)),
            pl.BlockSpec((1, nb, 2 * MOBA_HD), lambda b, p, i: (b, 0, p)),
        ],
        out_specs=pl.BlockSpec((blk, 2 * MOBA_HD), lambda b, p, i: (b * nb + i, p)),
        out_shape=jax.ShapeDtypeStruct((t, MOBA_WIDTH), BF16),
        scratch_shapes=[
            pltpu.VMEM((2, nb, blk), F32),
            pltpu.VMEM((2, MOBA_HD, blk), F32),
            pltpu.VMEM((2, 1, blk), F32),
            pltpu.VMEM((2, 1, blk), F32),
        ],
        compiler_params=_params("arbitrary", "arbitrary", "arbitrary"),
        name="moba",
    )(q4, k2d, v4, kmean)


def _extract_topk(s, payload, k, vals_ref, pay_ref, base):
    r = s.shape[0]
    row = lax.broadcasted_iota(I32, s.shape, 0)
    for j in range(k):
        mx = jnp.max(s, axis=0, keepdims=True)
        first = jnp.min(jnp.where(s == mx, row, r), axis=0, keepdims=True)
        pick = row == first
        vals_ref[base + j:base + j + 1, :] = mx
        pay_ref[base + j:base + j + 1, :] = jnp.sum(jnp.where(pick, payload, 0), axis=0, keepdims=True)
        s = jnp.where(pick, -jnp.inf, s)


def _mix_kernel(x_ref, att_ref, gm_ref, siga_ref, sigb_ref, kv_ref,
                wpa_ref, wpb_ref, wout_ref, gxa_ref, wxq_ref, wxo_ref, gffn_ref, wpq_ref, sk_ref,
                x2_ref, h3_ref, row8_ref, sh_ref, gate_ref,
                s1v_ref, s1i_ref, s2v_ref, s2i_ref, tv_ref, te_ref):
    tb = x_ref.shape[0]
    ya = jnp.dot(att_ref[...], wpa_ref[...], preferred_element_type=F32)
    yb = jnp.dot(gm_ref[...], wpb_ref[...], preferred_element_type=F32)
    merged = siga_ref[...].astype(F32) * ya + sigb_ref[...].astype(F32) * yb
    x1 = x_ref[...] + jnp.dot(merged.astype(BF16), wout_ref[...], preferred_element_type=F32)

    h2 = _rmsnorm(x1, gxa_ref[...]).astype(BF16)
    qx = jnp.dot(h2, wxq_ref[...], preferred_element_type=F32).astype(BF16)
    kv = kv_ref[...]
    heads = []
    for hd in range(XA_HEADS):
        qh = qx[:, hd * XA_HD:(hd + 1) * XA_HD]
        kh = kv[:, hd * XA_HD:(hd + 1) * XA_HD]
        vh = kv[:, XA_WIDTH + hd * XA_HD:XA_WIDTH + (hd + 1) * XA_HD]
        s = lax.dot_general(qh, kh, _NT, preferred_element_type=F32) * (XA_HD ** -0.5)
        mx = jnp.max(s, axis=-1, keepdims=True)
        p = jnp.exp(s - mx)
        p = p / jnp.sum(p, axis=-1, keepdims=True)
        heads.append(jnp.dot(p.astype(BF16), vh, preferred_element_type=F32))
    o = jnp.concatenate(heads, axis=1).astype(BF16)
    x2 = x1 + jnp.dot(o, wxo_ref[...], preferred_element_type=F32)
    x2_ref[...] = x2

    h3 = _rmsnorm(x2, gffn_ref[...])
    h3_ref[...] = h3
    pq = jnp.dot(h3.astype(BF16), wpq_ref[...], preferred_element_type=F32).astype(BF16)
    half = PEER_DQ // 2
    kk = PEER_TOPK
    key_row = lax.broadcasted_iota(I32, (PEER_NKEYS, tb), 0)
    for hd in range(PEER_HEADS):
        for part, (vref, iref) in enumerate(((s1v_ref, s1i_ref), (s2v_ref, s2i_ref))):
            c0 = (hd * 2 + part) * half
            st = lax.dot_general(sk_ref[hd * 2 + part], pq[:, c0:c0 + half], _NT,
                                 preferred_element_type=F32)
            _extract_topk(st, key_row, kk, vref, iref, 0)
        s1v, s1i, s2v, s2i = s1v_ref[...], s1i_ref[...], s2v_ref[...], s2i_ref[...]
        blocks_s, blocks_e = [], []
        i = 0
        while i < kk:
            n = kk // (i + 1)
            if n == 1:
                cnt = kk - i
                blocks_s.append(s1v[i:i + cnt] + s2v[0:1])
                blocks_e.append(s1i[i:i + cnt] * PEER_NKEYS + s2i[0:1])
                i += cnt
                continue
            rows = -(-n // SUBLANES) * SUBLANES
            cs = s1v[i:i + 1] + s2v[0:rows]
            ce = s1i[i:i + 1] * PEER_NKEYS + s2i[0:rows]
            if rows != n:
                keep = lax.broadcasted_iota(I32, (rows, tb), 0) < n
                cs = jnp.where(keep, cs, -jnp.inf)
            blocks_s.append(cs)
            blocks_e.append(ce)
            i += 1
        cand_s = jnp.concatenate(blocks_s, axis=0)
        cand_e = jnp.concatenate(blocks_e, axis=0)
        _extract_topk(cand_s, cand_e, kk, tv_ref, te_ref, hd * kk)

        tv = tv_ref[hd * kk:(hd + 1) * kk, :]
        ev = te_ref[hd * kk:(hd + 1) * kk, :]
        pe = jnp.exp(tv - jnp.max(tv, axis=0, keepdims=True))
        gate_ref[hd * kk:(hd + 1) * kk, :] = pe / jnp.sum(pe, axis=0, keepdims=True)
        row8_ref[hd * kk:(hd + 1) * kk, :] = (ev >> 1) * SUBLANES
        sh_ref[hd * kk:(hd + 1) * kk, :] = ((1 - (ev & 1)) * 16).astype(F32)


def _mix(x2d, att, gm, siga, sigb, kv, wpa, wpb, wout, gxa, wxq, wxo, gffn, wpq, sk, seq, n_mem):
    t, d = x2d.shape
    tb = ROW_TILE
    nt = t // tb
    per_b = seq // tb
    const2 = lambda i: (0, 0)
    rowspec = lambda w: pl.BlockSpec((tb, w), lambda i: (i, 0))
    colspec = pl.BlockSpec((PEER_HEADS * PEER_TOPK, tb), lambda i: (0, i))
    nsel = PEER_HEADS * PEER_TOPK
    return pl.pallas_call(
        _mix_kernel,
        grid=(nt,),
        in_specs=[
            rowspec(d), rowspec(MOBA_WIDTH), rowspec(GMLP_WIDTH), rowspec(d), rowspec(d),
            pl.BlockSpec((n_mem, kv.shape[1]), lambda i: (i // per_b, 0)),
            pl.BlockSpec(wpa.shape, const2), pl.BlockSpec(wpb.shape, const2),
            pl.BlockSpec(wout.shape, const2), pl.BlockSpec((1, d), const2),
            pl.BlockSpec(wxq.shape, const2), pl.BlockSpec(wxo.shape, const2),
            pl.BlockSpec((1, d), const2), pl.BlockSpec(wpq.shape, const2),
            pl.BlockSpec(sk.shape, lambda i: (0, 0, 0)),
        ],
        out_specs=[rowspec(d), rowspec(d), colspec, colspec, colspec],
        out_shape=[
            jax.ShapeDtypeStruct((t, d), F32),
            jax.ShapeDtypeStruct((t, d), F32),
            jax.ShapeDtypeStruct((nsel, t), I32),
            jax.ShapeDtypeStruct((nsel, t), F32),
            jax.ShapeDtypeStruct((nsel, t), F32),
        ],
        scratch_shapes=[
            pltpu.VMEM((PEER_TOPK, tb), F32), pltpu.VMEM((PEER_TOPK, tb), I32),
            pltpu.VMEM((PEER_TOPK, tb), F32), pltpu.VMEM((PEER_TOPK, tb), I32),
            pltpu.VMEM((nsel, tb), F32), pltpu.VMEM((nsel, tb), I32),
        ],
        compiler_params=_params("arbitrary"),
        name="mix_xattn",
    )(x2d, att, gm, siga, sigb, kv, wpa, wpb, wout, gxa, wxq, wxo, gffn, wpq, sk)


HIGH_HALF_MASK = -65536


def _expand_column(src_ref, t, dst_ref):
    lane = lax.broadcasted_iota(I32, src_ref.shape, 1)
    col = jnp.sum(jnp.where(lane == t, src_ref[...], 0.0), axis=1, keepdims=True)
    dst_ref[...] = jnp.broadcast_to(col.astype(dst_ref.dtype), dst_ref.shape)


def _expert_row(tbl_ref, row8, shexp_ref, k):
    w = tbl_ref[pl.ds(pl.multiple_of(row8, SUBLANES), SUBLANES), :]
    sh = jnp.broadcast_to(shexp_ref[k:k + 1, :], (SUBLANES, LANES))
    return pltpu.bitcast((w << sh) & HIGH_HALF_MASK, F32)


def _sublane_sums8(p):
    sub = lax.broadcasted_iota(I32, (SUBLANES, LANES), 0)

    def comb(x, y, s):
        lo = (sub & s) == 0
        return (jnp.where(lo, x, pltpu.roll(y, s, axis=0))
                + jnp.where(lo, pltpu.roll(x, SUBLANES - s, axis=0), y))

    z = [comb(p[0], p[4], 4), comb(p[2], p[6], 4), comb(p[1], p[5], 4), comb(p[3], p[7], 4)]
    w = [comb(z[0], z[1], 2), comb(z[2], z[3], 2)]
    return comb(w[0], w[1], 1)


def _chunk_rows(row8_ref, t, k0, dep):
    rows = [row8_ref[t + dep, k0 + j] for j in range(PEER_CHUNK)]
    return rows, rows[0] >> 31


def _peer_down_kernel(row8_ref, sh_ref, h_ref, gate_ref, tbl_ref, c_ref, acc_ref, shexp_a, shexp_b):
    nsel, tt = sh_ref.shape
    lane_id = lax.broadcasted_iota(I32, (SUBLANES, tt), 1)

    def dots(t, shexp_ref):
        h = h_ref[t]
        dep = 0
        for g0 in range(0, nsel, SUBLANES):
            rows, dep = _chunk_rows(row8_ref, t, g0, dep)
            prods = [_expert_row(tbl_ref, rows[j], shexp_ref, g0 + j) * h for j in range(SUBLANES)]
            col = jnp.sum(_sublane_sums8(prods), axis=1, keepdims=True)
            acc_ref[g0:g0 + SUBLANES, :] = jnp.where(lane_id == t, col, acc_ref[g0:g0 + SUBLANES, :])

    _expand_column(sh_ref, 0, shexp_a)

    def token_pair(i, carry):
        t = 2 * i
        _expand_column(sh_ref, t + 1, shexp_b)
        dots(t, shexp_a)
        _expand_column(sh_ref, jnp.minimum(t + 2, tt - 1), shexp_a)
        dots(t + 1, shexp_b)
        return carry

    lax.fori_loop(0, tt // 2, token_pair, 0)
    c_ref[...] = gate_ref[...] * jax.nn.gelu(acc_ref[...])


def _peer_gather_specs(nsel, tt, tbl):
    return dict(
        row8=pl.BlockSpec((tt, nsel), lambda i: (i, 0), memory_space=pltpu.SMEM),
        col=pl.BlockSpec((nsel, tt), lambda i: (0, i)),
        tok=pl.BlockSpec((tt, SUBLANES, LANES), lambda i: (i, 0, 0)),
        tbl=pl.BlockSpec(tbl.shape, lambda i: (0, 0), pipeline_mode=pl.Buffered(1)),
    )


def _peer_down(row8, sh_t, h3, gate_t, tbl):
    t, nsel = row8.shape
    tt = PEER_TILE
    s = _peer_gather_specs(nsel, tt, tbl)
    return pl.pallas_call(
        _peer_down_kernel,
        grid=(t // tt,),
        in_specs=[s["row8"], s["col"], s["tok"], s["col"], s["tbl"]],
        out_specs=s["col"],
        out_shape=jax.ShapeDtypeStruct((nsel, t), F32),
        scratch_shapes=[pltpu.VMEM((nsel, tt), F32), pltpu.VMEM((nsel, LANES), I32),
                        pltpu.VMEM((nsel, LANES), I32)],
        compiler_params=_params("arbitrary"),
        name="peer_down",
    )(row8, sh_t, h3, gate_t, tbl)


def _peer_up_kernel(row8_ref, sh_ref, c_ref, x_ref, tbl_ref, o_ref, shexp_a, cexp_a, shexp_b, cexp_b):
    nsel, tt = sh_ref.shape
    nacc = 4

    def weighted_sum(t, shexp_ref, cexp_ref):
        acc = [jnp.zeros((SUBLANES, LANES), F32) for _ in range(nacc)]
        dep = 0
        for k0 in range(0, nsel, PEER_CHUNK):
            rows, dep = _chunk_rows(row8_ref, t, k0, dep)
            for j in range(PEER_CHUNK):
                k = k0 + j
                c = jnp.broadcast_to(cexp_ref[k:k + 1, :], (SUBLANES, LANES))
                acc[k % nacc] = acc[k % nacc] + _expert_row(tbl_ref, rows[j], shexp_ref, k) * c
        o_ref[t] = x_ref[t] + ((acc[0] + acc[1]) + (acc[2] + acc[3]))

    def expand(t, shexp_ref, cexp_ref):
        _expand_column(sh_ref, t, shexp_ref)
        _expand_column(c_ref, t, cexp_ref)

    expand(0, shexp_a, cexp_a)

    def token_pair(i, carry):
        t = 2 * i
        expand(t + 1, shexp_b, cexp_b)
        weighted_sum(t, shexp_a, cexp_a)
        expand(jnp.minimum(t + 2, tt - 1), shexp_a, cexp_a)
        weighted_sum(t + 1, shexp_b, cexp_b)
        return carry

    lax.fori_loop(0, tt // 2, token_pair, 0)


def _peer_up(row8, sh_t, c_t, x3d, tbl):
    t, nsel = row8.shape
    tt = PEER_TILE
    s = _peer_gather_specs(nsel, tt, tbl)
    return pl.pallas_call(
        _peer_up_kernel,
        grid=(t // tt,),
        in_specs=[s["row8"], s["col"], s["col"], s["tok"], s["tbl"]],
        out_specs=s["tok"],
        out_shape=jax.ShapeDtypeStruct(x3d.shape, F32),
        scratch_shapes=[pltpu.VMEM((nsel, LANES), I32), pltpu.VMEM((nsel, LANES), F32),
                        pltpu.VMEM((nsel, LANES), I32), pltpu.VMEM((nsel, LANES), F32)],
        compiler_params=_params("arbitrary"),
        name="peer_up",
    )(row8, sh_t, c_t, x3d, tbl)


def _final_norm_kernel(x_ref, g_ref, o_ref):
    o_ref[...] = _rmsnorm(x_ref[...], g_ref[...])


def _final_norm(x2d, g):
    t, d = x2d.shape
    tb = 2 * ROW_TILE
    return pl.pallas_call(
        _final_norm_kernel,
        grid=(t // tb,),
        in_specs=[pl.BlockSpec((tb, d), lambda i: (i, 0)), pl.BlockSpec((1, d), lambda i: (0, 0))],
        out_specs=pl.BlockSpec((tb, d), lambda i: (i, 0)),
        out_shape=jax.ShapeDtypeStruct((t, d), F32),
        compiler_params=_params("arbitrary"),
        name="final_norm",
    )(x2d, g)


def _pack_expert_table(w):
    n, d = w.shape
    bits = lax.bitcast_convert_type(w.astype(BF16), jnp.uint16).astype(U32).reshape(n // 2, 2, d)
    packed = bits[:, 0, :] | (bits[:, 1, :] << 16)
    return lax.bitcast_convert_type(packed, I32).reshape(n // 2 * SUBLANES, d // SUBLANES)


def _rope_tables(seq):
    half = MOBA_HD // 2
    freqs = ROPE_THETA ** (-jnp.arange(half, dtype=F32) / half)
    ang = jnp.arange(seq, dtype=F32)[:, None] * freqs[None, :]
    cos = jnp.cos(ang)
    sin = jnp.sin(ang)
    cos_h = jnp.concatenate([cos, cos], axis=1)
    sin_h = jnp.concatenate([-sin, sin], axis=1)
    lane_reps = LANES // MOBA_HD
    return (jnp.tile(cos_h, (1, lane_reps)), jnp.tile(sin_h, (1, lane_reps)), cos_h.T, sin_h.T)


def _layer(x2d, kv, batch, seq, n_mem, norm_mix_g, w_in, moba_w_proj, gmlp_norm_g, gmlp_w_s, gmlp_b_s,
           gmlp_w_proj, w_out, norm_xa_g, xa_w_q, xa_w_o, norm_ffn_g, peer_w_q, peer_subkeys,
           peer_w_down, peer_w_up):
    t, d = x2d.shape
    c_q, c_k, c_v = 0, MOBA_WIDTH, 2 * MOBA_WIDTH
    c_gu = 3 * MOBA_WIDTH
    c_gv = c_gu + GMLP_WIDTH
    c_ga = c_gv + GMLP_WIDTH
    wtok = jnp.concatenate([w_in[:, c_k:c_v], w_in[:, c_ga:]], axis=1).astype(BF16)
    wfeat = jnp.concatenate([w_in[:, c_q:c_k], w_in[:, c_v:c_gu], w_in[:, c_gu:c_ga]], axis=1).T.astype(BF16)
    cos_t, sin_t, cos_f, sin_f = _rope_tables(seq)
    causal = jnp.tril(jnp.ones((GMLP_CHUNK, GMLP_CHUNK), dtype=bool))
    wm = jnp.where(causal[None], gmlp_w_s, 0.0).astype(BF16)
    gng = jnp.broadcast_to(gmlp_norm_g[:, None], (GMLP_WIDTH, ROW_TILE))

    q4, k2d, v4, kmean, gm, siga, sigb = _in_proj(
        x2d, norm_mix_g[None, :], wtok, wfeat, cos_t, sin_t, cos_f, sin_f, gng, wm, gmlp_b_s, seq)
    att = _moba(q4, k2d, v4, kmean.reshape(batch, seq // MOBA_BLOCK, MOBA_WIDTH), batch, seq)

    sk = peer_subkeys.reshape(PEER_HEADS * 2, PEER_NKEYS, PEER_DQ // 2).astype(BF16)
    x2, h3, row8_t, sh_t, gate_t = _mix(
        x2d, att, gm, siga, sigb, kv,
        moba_w_proj.astype(BF16), gmlp_w_proj.astype(BF16), w_out.astype(BF16), norm_xa_g[None, :],
        xa_w_q.astype(BF16), xa_w_o.astype(BF16), norm_ffn_g[None, :], peer_w_q.astype(BF16), sk,
        seq, n_mem)

    row8 = row8_t.T
    c_t = _peer_down(row8, sh_t, h3.reshape(t, SUBLANES, d // SUBLANES), gate_t,
                     _pack_expert_table(peer_w_down))
    x3 = _peer_up(row8, sh_t, c_t, x2.reshape(t, SUBLANES, d // SUBLANES), _pack_expert_table(peer_w_up))
    return x3.reshape(t, d)


def kernel(x, mem, norm_mix_g, w_in, moba_w_proj, gmlp_norm_g, gmlp_w_s, gmlp_b_s, gmlp_w_proj, w_out,
           norm_xa_g, norm_mem_g, xa_w_q, xa_w_kv, xa_w_o, norm_ffn_g, peer_w_q, peer_subkeys,
           peer_w_down, peer_w_up, final_g):
    batch, seq, d = x.shape
    n_mem = mem.shape[1]
    depth = w_in.shape[0]
    assert seq % MOBA_BLOCK == 0 and (batch * seq) % PEER_TILE == 0 and d == SUBLANES * LANES
    x2d = x.reshape(batch * seq, d)
    mem2d = mem.reshape(batch * n_mem, d)
    for l in range(depth):
        kv = _mem_kv(mem2d, norm_mem_g[l][None, :], xa_w_kv[l].astype(BF16), n_mem)
        x2d = _layer(x2d, kv, batch, seq, n_mem, norm_mix_g[l], w_in[l], moba_w_proj[l], gmlp_norm_g[l],
                     gmlp_w_s[l], gmlp_b_s[l], gmlp_w_proj[l], w_out[l], norm_xa_g[l], xa_w_q[l],
                     xa_w_o[l], norm_ffn_g[l], peer_w_q[l], peer_subkeys[l], peer_w_down[l], peer_w_up[l])
    return _final_norm(x2d, final_g[None, :]).reshape(batch, seq, d)
```
